```python
import math
import jax, jax.numpy as jnp
from jax import lax
import numpy as np

D_MODEL = 2048
BATCH = 8
SEQ = 4096
DEPTH = 2

N_A = (DEPTH + 1) // 2
N_B = DEPTH - N_A
HEAD_DIM = 128
N_HEADS = D_MODEL // HEAD_DIM
NSA_KV_GROUPS = 4
B_KV_HEADS = 4
CMP_BLOCK = 32
CMP_STRIDE = 16
CMP_HIDDEN = HEAD_DIM
SEL_BLOCK = 64
SEL_COUNT = 16
SEL_FORCE = 1e4
NSA_WINDOW = 512
NSA_Q_CHUNK = 32
DIL_PAIRS = ((128, 1), (512, 4), (2048, 16))
D_FF = 5632
CONV_WIDTH = 3
REL_BUCKETS = 32
REL_MAX_DIST = 4096
QUERY_BLOCK = 128
EPS = 1e-6
SCALE = HEAD_DIM ** -0.5
A_IN_COLS = N_HEADS * HEAD_DIM + 6 * NSA_KV_GROUPS * HEAD_DIM + 3 * N_HEADS

kernel_name = "yoco_nsa_dilated_convffn_trunk"


def rmsnorm(x, g):
    xf = x.astype(jnp.float32)
    y = xf * lax.rsqrt(jnp.mean(xf * xf, axis=-1, keepdims=True) + EPS) * g.astype(jnp.float32)
    return y.astype(x.dtype)


def head_rms(x, g):
    return rmsnorm(x, g)


def t5_bucket(dist):
    n = jnp.maximum(jnp.asarray(dist, jnp.int32), 0)
    exact = REL_BUCKETS // 2
    log_ratio = jnp.log(jnp.maximum(n, 1).astype(jnp.float32) / exact) / math.log(REL_MAX_DIST / exact)
    large = exact + (log_ratio * (REL_BUCKETS - exact)).astype(jnp.int32)
    return jnp.where(n < exact, n, jnp.minimum(large, REL_BUCKETS - 1))


def masked_softmax(logits, mask):
    s = jnp.where(mask, logits.astype(jnp.float32), -jnp.inf)
    m = jnp.max(s, axis=-1, keepdims=True)
    m = jnp.where(jnp.isfinite(m), m, 0.0)
    e = jnp.exp(s - m)
    den = jnp.sum(e, axis=-1, keepdims=True)
    p = e / jnp.maximum(den, 1e-30)
    lse = (m + jnp.log(den))[..., 0]
    return p, lse


def banded_attention(q, k, v, rel_table, n_back, dilation):
    n, length, h, dh = q.shape
    g = k.shape[2]
    r = h // g
    qb = math.gcd(length, QUERY_BLOCK)
    nb = length // qb
    span = qb + n_back
    pad = ((0, 0), (n_back, 0), (0, 0), (0, 0))
    k_pad, v_pad = jnp.pad(k, pad), jnp.pad(v, pad)
    idx = np.arange(nb)[:, None] * qb + np.arange(span)[None, :]
    k_blk, v_blk = k_pad[:, idx], v_pad[:, idx]
    q_blk = q.reshape(n, nb, qb, g, r, dh)
    logits = jnp.einsum('nbqgrd,nbsgd->nbgrqs', q_blk, k_blk).astype(jnp.float32) * SCALE
    steps = np.arange(qb)[:, None] + n_back - np.arange(span)[None, :]
    key_pos = idx - n_back
    mask = (steps >= 0) & (steps <= n_back) & (key_pos[:, None, :] >= 0)
    bias = rel_table[t5_bucket(steps * dilation)]
    bias = bias.reshape(qb, span, g, r).transpose(2, 3, 0, 1)
    p, lse = masked_softmax(logits + bias, mask[None, :, None, None])
    out = jnp.einsum('nbgrqs,nbsgd->nbqgrd', p, v_blk)
    return out.reshape(n, length, h, dh), lse.transpose(0, 1, 4, 2, 3).reshape(n, length, h)


def compress(kv, pos_emb, w1, w2):
    b, s, g, dh = kv.shape
    n_cmp = (s - CMP_BLOCK) // CMP_STRIDE + 1
    idx = np.arange(n_cmp)[:, None] * CMP_STRIDE + np.arange(CMP_BLOCK)[None, :]
    blocks = kv[:, idx] + pos_emb[:, None, :]
    flat = blocks.transpose(0, 1, 3, 2, 4).reshape(b, n_cmp, g, CMP_BLOCK * dh)
    return jax.nn.gelu(flat @ w1) @ w2


def block_overlap(n_cmp, n_blk):
    start = np.arange(n_cmp)[:, None] * CMP_STRIDE
    blk_start = np.arange(n_blk)[None, :] * SEL_BLOCK
    ov = np.minimum(start + CMP_BLOCK, blk_start + SEL_BLOCK) - np.maximum(start, blk_start)
    return np.maximum(ov, 0).astype(np.float32) / CMP_BLOCK


def nsa_compressed_selected(q, k_cmp, v_cmp, k_slc, v_slc, rel_table):
    b, s, h, dh = q.shape
    g = k_cmp.shape[2]
    r = h // g
    n_cmp = k_cmp.shape[1]
    n_blk = s // SEL_BLOCK
    n_sel = min(SEL_COUNT, n_blk)
    qc = NSA_Q_CHUNK
    n_chunk = s // qc
    comp_end = (np.arange(n_cmp) * CMP_STRIDE + CMP_BLOCK - 1).astype(np.int32)
    overlap = jnp.asarray(block_overlap(n_cmp, n_blk))
    k_blocks = k_slc.reshape(b, n_blk, SEL_BLOCK, g, dh).transpose(0, 3, 1, 2, 4)
    v_blocks = v_slc.reshape(b, n_blk, SEL_BLOCK, g, dh).transpose(0, 3, 1, 2, 4)
    table_g = rel_table.reshape(REL_BUCKETS, g, r).transpose(1, 0, 2)
    b_idx = jnp.arange(b)[:, None, None, None]
    g_idx = jnp.arange(g)[None, :, None, None]
    blk = jnp.arange(n_blk, dtype=jnp.int32)

    def chunk(args):
        q_c, t0 = args
        t = t0 + jnp.arange(qc, dtype=jnp.int32)
        logit_c = jnp.einsum('bqgrd,bngd->bgrqn', q_c, k_cmp).astype(jnp.float32) * SCALE
        dist_c = t[:, None] - comp_end[None, :]
        bias_c = rel_table[t5_bucket(dist_c)].reshape(qc, n_cmp, g, r).transpose(2, 3, 0, 1)
        p_c, _ = masked_softmax(logit_c + bias_c, dist_c >= 0)
        o_c = jnp.einsum('bgrqn,bngd->bqgrd', p_c, v_cmp)
        imp = jnp.einsum('bgrqn,nj->bgqj', p_c, overlap)
        cur = (t // SEL_BLOCK)[:, None]
        forced = (blk == 0) | (blk == cur) | (blk == cur - 1)
        score = jnp.where(blk * SEL_BLOCK <= t[:, None], imp + jnp.where(forced, SEL_FORCE, 0.0), -jnp.inf)
        _, sel = lax.top_k(score, n_sel)
        k_s = k_blocks[b_idx, g_idx, sel].reshape(b, g, qc, n_sel * SEL_BLOCK, dh)
        v_s = v_blocks[b_idx, g_idx, sel].reshape(b, g, qc, n_sel * SEL_BLOCK, dh)
        pos = (sel[..., None] * SEL_BLOCK + jnp.arange(SEL_BLOCK, dtype=jnp.int32)).reshape(b, g, qc, -1)
        dist_s = t[:, None] - pos
        logit_s = jnp.einsum('bqgrd,bgqsd->bgrqs', q_c, k_s).astype(jnp.float32) * SCALE
        bias_s = jnp.moveaxis(table_g[g_idx, t5_bucket(dist_s)], -1, 2)
        p_s, _ = masked_softmax(logit_s + bias_s, (dist_s >= 0)[:, :, None])
        o_s = jnp.einsum('bgrqs,bgqsd->bqgrd', p_s, v_s)
        return o_c, o_s

    q_chunks = q.reshape(b, n_chunk, qc, g, r, dh).swapaxes(0, 1)
    starts = jnp.arange(n_chunk, dtype=jnp.int32) * qc
    o_c, o_s = lax.map(chunk, (q_chunks, starts))
    return (o_c.swapaxes(0, 1).reshape(b, s, h, dh), o_s.swapaxes(0, 1).reshape(b, s, h, dh))


def nsa_mixer(h, rel_table, w_in, q_gain, k_gains, cmp_pos, cmp_w1, cmp_w2, w_out):
    b, s, _ = h.shape
    g, dh = NSA_KV_GROUPS, HEAD_DIM
    q_dim, kv_dim = N_HEADS * dh, g * dh
    proj = h @ w_in
    q = head_rms(proj[..., :q_dim].reshape(b, s, N_HEADS, dh), q_gain)
    kvs = proj[..., q_dim:q_dim + 6 * kv_dim].reshape(b, s, 6, g, dh)
    gates = jax.nn.sigmoid(proj[..., q_dim + 6 * kv_dim:].astype(jnp.float32)).reshape(b, s, N_HEADS, 3)
    k_cmp = head_rms(compress(kvs[:, :, 0], cmp_pos[0], cmp_w1[0], cmp_w2[0]), k_gains[0])
    v_cmp = compress(kvs[:, :, 1], cmp_pos[1], cmp_w1[1], cmp_w2[1])
    k_slc, v_slc = head_rms(kvs[:, :, 2], k_gains[1]), kvs[:, :, 3]
    k_win, v_win = head_rms(kvs[:, :, 4], k_gains[2]), kvs[:, :, 5]
    o_cmp, o_slc = nsa_compressed_selected(q, k_cmp, v_cmp, k_slc, v_slc, rel_table)
    o_win, _ = banded_attention(q, k_win, v_win, rel_table, NSA_WINDOW - 1, 1)
    o = gates[..., 0:1] * o_cmp + gates[..., 1:2] * o_slc + gates[..., 2:3] * o_win
    return o.reshape(b, s, q_dim).astype(h.dtype) @ w_out


def shared_kv(x, g_norm, w_kv, k_gain):
    b, s, _ = x.shape
    kv = (rmsnorm(x, g_norm) @ w_kv).reshape(b, s, 2, B_KV_HEADS, HEAD_DIM)
    return head_rms(kv[:, :, 0], k_gain), kv[:, :, 1]


def to_residues(x, d):
    b, s = x.shape[:2]
    y = jnp.moveaxis(x.reshape(b, s // d, d, *x.shape[2:]), 2, 1)
    return y.reshape(b * d, s // d, *x.shape[2:])


def from_residues(x, b, d):
    n, length = x.shape[:2]
    y = jnp.moveaxis(x.reshape(b, d, length, *x.shape[2:]), 1, 2)
    return y.reshape(b, length * d, *x.shape[2:])


def dilated_mixer(h, k_sh, v_sh, rel_table, w_q, q_gains, w_out):
    b, s, _ = h.shape
    q = (h @ w_q).reshape(b, s, len(DIL_PAIRS), N_HEADS, HEAD_DIM)
    outs, lses = [], []
    for gi, (window, dil) in enumerate(DIL_PAIRS):
        qg = to_residues(head_rms(q[:, :, gi], q_gains[gi]), dil)
        o, lse = banded_attention(qg, to_residues(k_sh, dil), to_residues(v_sh, dil), rel_table, window // dil, dil)
        outs.append(from_residues(o, b, dil))
        lses.append(from_residues(lse, b, dil))
    alpha = jax.nn.softmax(jnp.stack(lses), axis=0)
    o = jnp.einsum('gbsh,gbshd->bshd', alpha, jnp.stack(outs))
    return o.reshape(b, s, N_HEADS * HEAD_DIM).astype(h.dtype) @ w_out


def conv_ffn(x, g, w_up, conv_w, w_down):
    u = rmsnorm(x, g) @ w_up
    u = lax.conv_general_dilated(u, conv_w[:, None, :], window_strides=(1,),
                                 padding=[(CONV_WIDTH - 1, 0)],
                                 dimension_numbers=('NWC', 'WIO', 'NWC'),
                                 feature_group_count=u.shape[-1])
    gate, val = jnp.split(u, 2, axis=-1)
    return (jax.nn.silu(gate) * val) @ w_down


def setup_inputs(seed: int = 0) -> dict:
    key = jax.random.key(seed)
    ks = jax.random.split(key, 20)
    f32 = jnp.float32

    def nrm(k, shape, fan_in):
        return jax.random.normal(k, shape, f32) * (fan_in ** -0.5)

    def gain(k, shape):
        return 1.0 + 0.05 * jax.random.normal(k, shape, f32)

    dh = HEAD_DIM
    return {
        'x': jax.random.normal(ks[0], (BATCH, SEQ, D_MODEL), f32),
        'rel_bias': 0.3 * jax.random.normal(ks[1], (REL_BUCKETS, N_HEADS), f32),
        'attn_norm': gain(ks[2], (DEPTH, D_MODEL)),
        'ffn_norm': gain(ks[3], (DEPTH, D_MODEL)),
        'a_w_in': nrm(ks[4], (N_A, D_MODEL, A_IN_COLS), D_MODEL),
        'a_q_norm': gain(ks[5], (N_A, dh)),
        'a_k_norm': gain(ks[6], (N_A, 3, dh)),
        'a_cmp_pos': 0.1 * jax.random.normal(ks[7], (N_A, 2, CMP_BLOCK, dh), f32),
        'a_cmp_w1': nrm(ks[8], (N_A, 2, CMP_BLOCK * dh, CMP_HIDDEN), CMP_BLOCK * dh),
        'a_cmp_w2': nrm(ks[9], (N_A, 2, CMP_HIDDEN, dh), CMP_HIDDEN),
        'a_w_out': nrm(ks[10], (N_A, N_HEADS * dh, D_MODEL), N_HEADS * dh),
        'kv_norm': gain(ks[11], (D_MODEL,)),
        'kv_w': nrm(ks[12], (D_MODEL, 2 * B_KV_HEADS * dh), D_MODEL),
        'kv_k_norm': gain(ks[13], (dh,)),
        'b_w_q': nrm(ks[14], (N_B, D_MODEL, len(DIL_PAIRS) * N_HEADS * dh), D_MODEL),
        'b_q_norm': gain(ks[15], (N_B, len(DIL_PAIRS), dh)),
        'b_w_out': nrm(ks[16], (N_B, N_HEADS * dh, D_MODEL), N_HEADS * dh),
        'ffn_w_up': nrm(ks[17], (DEPTH, D_MODEL, 2 * D_FF), D_MODEL),
        'ffn_conv': nrm(ks[18], (DEPTH, CONV_WIDTH, 2 * D_FF), CONV_WIDTH),
        'ffn_w_down': nrm(ks[19], (DEPTH, D_FF, D_MODEL), D_FF),
    }


def reference(x, rel_bias, attn_norm, ffn_norm, a_w_in, a_q_norm, a_k_norm, a_cmp_pos, a_cmp_w1,
              a_cmp_w2, a_w_out, kv_norm, kv_w, kv_k_norm, b_w_q, b_q_norm, b_w_out,
              ffn_w_up, ffn_conv, ffn_w_down):
    k_sh, v_sh = None, None
    for layer in range(DEPTH):
        h = rmsnorm(x, attn_norm[layer])
        if layer < N_A:
            mix = nsa_mixer(h, rel_bias, a_w_in[layer], a_q_norm[layer], a_k_norm[layer],
                            a_cmp_pos[layer], a_cmp_w1[layer], a_cmp_w2[layer], a_w_out[layer])
        else:
            if layer == N_A:
                k_sh, v_sh = shared_kv(x, kv_norm, kv_w, kv_k_norm)
            j = layer - N_A
            mix = dilated_mixer(h, k_sh, v_sh, rel_bias, b_w_q[j], b_q_norm[j], b_w_out[j])
        x = x + mix.astype(x.dtype)
        x = x + conv_ffn(x, ffn_norm[layer], ffn_w_up[layer], ffn_conv[layer], ffn_w_down[layer]).astype(x.dtype)
    return x
```

```python
import functools
import math

import numpy as np
import jax
import jax.numpy as jnp
from jax import lax
from jax.experimental import pallas as pl
from jax.experimental.pallas import tpu as pltpu

F32 = jnp.float32
BF16 = jnp.bfloat16

HEAD_DIM = 128
N_HEADS = 16
KV_GROUPS = 4
HEADS_PER_GROUP = N_HEADS // KV_GROUPS
CMP_BLOCK = 32
CMP_STRIDE = 16
SEL_BLOCK = 64
SEL_COUNT = 16
SEL_FORCE = 1e4
NSA_WINDOW = 512
DIL_PAIRS = ((128, 1), (512, 4), (2048, 16))
D_FF = 5632
REL_BUCKETS = 32
REL_MAX_DIST = 4096
EPS = 1e-6
SCALE = HEAD_DIM ** -0.5
LOG2E = 1.4426950408889634
NEG = -1e30

LANES = 128
SUBLANES = 8
BF16_SUBLANES = 16
MXU_DEPTH = 256
VMEM_LIMIT = 56 * 1024 * 1024

TQ = 128
TK = 256
CONV_HALO = BF16_SUBLANES


def _cparams(sem):
    return pltpu.CompilerParams(dimension_semantics=sem, vmem_limit_bytes=VMEM_LIMIT)


def _t5_bucket(dist):
    n = jnp.maximum(jnp.asarray(dist, jnp.int32), 0)
    exact = REL_BUCKETS // 2
    log_ratio = jnp.log(jnp.maximum(n, 1).astype(jnp.float32) / exact) / math.log(REL_MAX_DIST / exact)
    large = exact + (log_ratio * (REL_BUCKETS - exact)).astype(jnp.int32)
    return jnp.where(n < exact, n, jnp.minimum(large, REL_BUCKETS - 1))


def _bias_expand_kernel(tab_ref, idx_ref, o_ref, *, tq):
    g = pl.program_id(0)
    idx = idx_ref[...]
    for hl in range(HEADS_PER_GROUP):
        acc = jnp.full(idx.shape, NEG, F32)
        for b in range(REL_BUCKETS):
            acc = jnp.where(idx == b, tab_ref[HEADS_PER_GROUP * g + hl, b] * LOG2E, acc)
        o_ref[0, :, hl * tq:(hl + 1) * tq] = acc


def _row_chunk(rows, cap=512):
    best = SUBLANES
    for c in range(SUBLANES, min(rows, cap) + 1, SUBLANES):
        if rows % c == 0:
            best = c
    return best


def _bias_strip(rel_t, dist, valid, tq):
    idx = jnp.where(valid, _t5_bucket(dist), -1).astype(jnp.int32)
    rows = idx.shape[0]
    rc = _row_chunk(rows)
    return pl.pallas_call(
        functools.partial(_bias_expand_kernel, tq=tq),
        grid=(KV_GROUPS, rows // rc),
        in_specs=[
            pl.BlockSpec(memory_space=pltpu.SMEM),
            pl.BlockSpec((rc, tq), lambda g, r: (r, 0)),
        ],
        out_specs=pl.BlockSpec((1, rc, HEADS_PER_GROUP * tq), lambda g, r: (g, r, 0)),
        out_shape=jax.ShapeDtypeStruct((KV_GROUPS, rows, HEADS_PER_GROUP * tq), F32),
        compiler_params=_cparams(("arbitrary", "arbitrary")),
        name="bias_strip",
    )(rel_t, idx)


def _toeplitz_dist(rows, tq, center):
    r = np.arange(rows, dtype=np.int32)[:, None]
    q = np.arange(tq, dtype=np.int32)[None, :]
    return q - r + center


def _rms(x, gamma):
    ms = jnp.mean(x * x, axis=-1, keepdims=True)
    return x * lax.rsqrt(ms + EPS) * gamma


def _norm_proj_kernel(x_ref, g_ref, w_ref, cg_ref, cf_ref, o_ref, h_ref, *, epilogue, tn):
    @pl.when(pl.program_id(1) == 0)
    def _():
        h_ref[...] = _rms(x_ref[...], g_ref[...]).astype(BF16)

    y = jnp.dot(h_ref[...], w_ref[...], preferred_element_type=F32)
    if epilogue == "headnorm":
        for j in range(tn // HEAD_DIM):
            sl = slice(j * HEAD_DIM, (j + 1) * HEAD_DIM)
            yj = y[:, sl]
            ms = jnp.mean(yj * yj, axis=-1, keepdims=True)
            scale = jnp.where(cf_ref[:, sl] > 0, lax.rsqrt(ms + EPS) * cg_ref[:, sl], 1.0)
            o_ref[:, sl] = (yj * scale).astype(o_ref.dtype)
    elif epilogue == "sigmoid":
        o_ref[...] = (1.0 / (1.0 + jnp.exp(-y))).astype(o_ref.dtype)
    else:
        o_ref[...] = y.astype(o_ref.dtype)


def _norm_proj(x, gamma, w, col_gain, col_flag, *, epilogue, out_dtype, tm, tn, name):
    m, d = x.shape
    n = w.shape[1]
    assert m % tm == 0 and n % tn == 0, (m, n, tm, tn)
    return pl.pallas_call(
        functools.partial(_norm_proj_kernel, epilogue=epilogue, tn=tn),
        grid=(m // tm, n // tn),
        in_specs=[
            pl.BlockSpec((tm, d), lambda i, j: (i, 0)),
            pl.BlockSpec((1, d), lambda i, j: (0, 0)),
            pl.BlockSpec((d, tn), lambda i, j: (0, j)),
            pl.BlockSpec((1, tn), lambda i, j: (0, j)),
            pl.BlockSpec((1, tn), lambda i, j: (0, j)),
        ],
        out_specs=pl.BlockSpec((tm, tn), lambda i, j: (i, j)),
        out_shape=jax.ShapeDtypeStruct((m, n), out_dtype),
        scratch_shapes=[pltpu.VMEM((tm, d), BF16)],
        compiler_params=_cparams(("parallel", "arbitrary")),
        name=name,
    )(x, gamma.reshape(1, d), w, col_gain.reshape(1, n), col_flag.reshape(1, n))


def _mm_res_kernel(a_ref, w_ref, r_ref, o_ref, *, trans_a):
    if trans_a:
        y = lax.dot_general(a_ref[0], w_ref[...], (((0,), (0,)), ((), ())), preferred_element_type=F32)
    else:
        y = jnp.dot(a_ref[...], w_ref[...], preferred_element_type=F32)
    o_ref[...] = r_ref[...] + y


def _mm_res(a, w, res, *, trans_a, tm, tn, name):
    m, n = res.shape
    k = w.shape[0]
    assert m % tm == 0 and n % tn == 0
    if trans_a:
        per_b = a.shape[2] // tm
        a_spec = pl.BlockSpec((1, k, tm), lambda j, i: (i // per_b, 0, i % per_b))
    else:
        a_spec = pl.BlockSpec((tm, k), lambda j, i: (i, 0))
    return pl.pallas_call(
        functools.partial(_mm_res_kernel, trans_a=trans_a),
        grid=(n // tn, m // tm),
        in_specs=[
            a_spec,
            pl.BlockSpec((k, tn), lambda j, i: (0, j)),
            pl.BlockSpec((tm, tn), lambda j, i: (i, j)),
        ],
        out_specs=pl.BlockSpec((tm, tn), lambda j, i: (i, j)),
        out_shape=jax.ShapeDtypeStruct((m, n), F32),
        compiler_params=_cparams(("arbitrary", "arbitrary")),
        name=name,
    )(a, w, res)


def _ffn_up_kernel(x_ref, xh_ref, g_ref, wg_ref, wv_ref, cg_ref, cv_ref, o_ref, h_ref, *, tm, seq):
    @pl.when(pl.program_id(1) == 0)
    def _():
        h_ref[CONV_HALO:, :] = _rms(x_ref[...], g_ref[...]).astype(BF16)
        keep = jnp.where(lax.rem(pl.program_id(0) * tm, seq) == 0, 0.0, 1.0)
        h_ref[0:CONV_HALO, :] = (_rms(xh_ref[...], g_ref[...]) * keep).astype(BF16)

    h = h_ref[...]

    def conv(u, c_ref):
        return (c_ref[0:1, :] * u[CONV_HALO - 2:CONV_HALO - 2 + tm]
                + c_ref[1:2, :] * u[CONV_HALO - 1:CONV_HALO - 1 + tm]
                + c_ref[2:3, :] * u[CONV_HALO:CONV_HALO + tm])

    gate = conv(jnp.dot(h, wg_ref[...], preferred_element_type=F32), cg_ref)
    val = conv(jnp.dot(h, wv_ref[...], preferred_element_type=F32), cv_ref)
    o_ref[...] = (gate * (1.0 / (1.0 + jnp.exp(-gate))) * val).astype(o_ref.dtype)


def _ffn_up(x, gamma, w_up, conv_w, *, seq, tm, tn, name):
    m, d = x.shape
    nf = w_up.shape[1] // 2
    assert m % tm == 0 and nf % tn == 0 and seq % tm == 0
    nt = nf // tn
    hb = tm // CONV_HALO
    conv_p = jnp.pad(conv_w, ((0, SUBLANES - conv_w.shape[0]), (0, 0)))
    return pl.pallas_call(
        functools.partial(_ffn_up_kernel, tm=tm, seq=seq),
        grid=(m // tm, nt),
        in_specs=[
            pl.BlockSpec((tm, d), lambda i, j: (i, 0)),
            pl.BlockSpec((CONV_HALO, d), lambda i, j: (jnp.maximum(i * hb - 1, 0), 0)),
            pl.BlockSpec((1, d), lambda i, j: (0, 0)),
            pl.BlockSpec((d, tn), lambda i, j: (0, j)),
            pl.BlockSpec((d, tn), lambda i, j: (0, nt + j)),
            pl.BlockSpec((SUBLANES, tn), lambda i, j: (0, j)),
            pl.BlockSpec((SUBLANES, tn), lambda i, j: (0, nt + j)),
        ],
        out_specs=pl.BlockSpec((tm, tn), lambda i, j: (i, j)),
        out_shape=jax.ShapeDtypeStruct((m, nf), BF16),
        scratch_shapes=[pltpu.VMEM((tm + CONV_HALO, d), BF16)],
        compiler_params=_cparams(("parallel", "arbitrary")),
        name=name,
    )(x, x, gamma.reshape(1, d), w_up, w_up, conv_p, conv_p)


def _compress_kernel(x_ref, pos_ref, w1_ref, w2_ref, cg_ref, cf_ref, o_ref):
    w1 = w1_ref[0]
    c = jnp.dot(x_ref[0, 0, 0], w1, preferred_element_type=F32)
    pc = jnp.dot(pos_ref[0], w1, preferred_element_type=F32)
    ncp = c.shape[0]
    lo = c[:, :HEAD_DIM]
    hi = pltpu.roll(c[:, HEAD_DIM:], ncp - 1, 0)
    hid = lo + hi + pc[0:1, :HEAD_DIM] + pc[1:2, HEAD_DIM:]
    act = 0.5 * hid * (1.0 + jnp.tanh(math.sqrt(2.0 / math.pi) * (hid + 0.044715 * hid * hid * hid)))
    y = jnp.dot(act.astype(BF16), w2_ref[0], preferred_element_type=F32)
    ms = jnp.mean(y * y, axis=-1, keepdims=True)
    scale = jnp.where(cf_ref[0] > 0, lax.rsqrt(ms + EPS) * cg_ref[0], 1.0)
    o_ref[0, 0, 0] = (y * scale).astype(o_ref.dtype)


def _compress(xr, pos2, w1cat, w2, gain, flag):
    _, b, g, ncp, kd = xr.shape
    return pl.pallas_call(
        _compress_kernel,
        grid=(2, b, g),
        in_specs=[
            pl.BlockSpec((1, 1, 1, ncp, kd), lambda j, i, k: (j, i, k, 0, 0)),
            pl.BlockSpec((1, SUBLANES, kd), lambda j, i, k: (j, 0, 0)),
            pl.BlockSpec((1, kd, 2 * HEAD_DIM), lambda j, i, k: (j, 0, 0)),
            pl.BlockSpec((1, HEAD_DIM, HEAD_DIM), lambda j, i, k: (j, 0, 0)),
            pl.BlockSpec((1, 1, HEAD_DIM), lambda j, i, k: (j, 0, 0)),
            pl.BlockSpec((1, 1, HEAD_DIM), lambda j, i, k: (j, 0, 0)),
        ],
        out_specs=pl.BlockSpec((1, 1, 1, ncp, HEAD_DIM), lambda j, i, k: (j, i, k, 0, 0)),
        out_shape=jax.ShapeDtypeStruct((2, b, g, ncp, HEAD_DIM), BF16),
        compiler_params=_cparams(("arbitrary", "arbitrary", "arbitrary")),
        name="nsa_compress",
    )(xr, pos2, w1cat, w2, gain, flag)


def _softmax_cols(s):
    m = jnp.max(s, axis=0, keepdims=True)
    p = jnp.exp2(s - m)
    return m, p, jnp.sum(p, axis=0, keepdims=True)


def _block_rank(score, nb):
    tq = score.shape[1]
    groups = [score[SUBLANES * v:SUBLANES * (v + 1), :] for v in range(nb // SUBLANES)]
    cnt = [jnp.zeros((SUBLANES, tq), F32) for _ in groups]
    row = lax.broadcasted_iota(jnp.int32, (SUBLANES, tq), 0)
    for i in range(nb):
        ri = jnp.broadcast_to(score[i:i + 1, :], (SUBLANES, tq))
        vi = i // SUBLANES
        for v, gv in enumerate(groups):
            ge = jnp.where(ri >= gv, 1.0, 0.0)
            gt = jnp.where(ri > gv, 1.0, 0.0)
            if v > vi:
                cnt[v] = cnt[v] + ge
            elif v < vi:
                cnt[v] = cnt[v] + gt
            else:
                cnt[v] = cnt[v] + jnp.where(row > (i % SUBLANES), ge, gt)
    return jnp.concatenate(cnt, axis=0)


def _nsa_kernel(qT_ref, gT_ref, kc_ref, vcT_ref, ks_ref, vsT_ref, kw_ref, vwT_ref,
                sc_ref, ss_ref, sw_ref, ovT_ref, o_ref, rhs_ref, acc_ref, *, seq, nb, ncp):
    hq = HEADS_PER_GROUP * TQ
    qi = pl.program_id(2)
    q0 = qi * TQ

    for h in range(HEADS_PER_GROUP):
        rhs_ref[0:HEAD_DIM, h * TQ:(h + 1) * TQ] = qT_ref[0, h * HEAD_DIM:(h + 1) * HEAD_DIM, :]
    rhs_ref[HEAD_DIM + nb:, :] = jnp.zeros((MXU_DEPTH - HEAD_DIM - nb, hq), BF16)
    rhs_q = rhs_ref[0:HEAD_DIM, :]

    off_c = pl.multiple_of((seq - TQ) // CMP_STRIDE - qi * (TQ // CMP_STRIDE), SUBLANES)
    s = jnp.dot(kc_ref[0, 0], rhs_q, preferred_element_type=F32) + sc_ref[0, pl.ds(off_c, ncp), :]
    m, p, l = _softmax_cols(s)
    pn = p * jnp.where(m > 0.5 * NEG, 1.0 / l, 0.0)
    o_cmp = jnp.dot(vcT_ref[0, 0], pn.astype(BF16), preferred_element_type=F32)

    psum = pn[:, 0:TQ]
    for h in range(1, HEADS_PER_GROUP):
        psum = psum + pn[:, h * TQ:(h + 1) * TQ]
    p_hi = psum.astype(BF16)
    p_lo = (psum - p_hi.astype(F32)).astype(BF16)
    ovT = ovT_ref[...]
    imp = (jnp.dot(ovT, p_hi, preferred_element_type=F32)
           + jnp.dot(ovT, p_lo, preferred_element_type=F32))
    t = q0 + lax.broadcasted_iota(jnp.int32, (nb, TQ), 1)
    blk = lax.broadcasted_iota(jnp.int32, (nb, TQ), 0)
    causal = blk * SEL_BLOCK <= t
    cur = lax.shift_right_logical(t, int(math.log2(SEL_BLOCK)))
    forced = (blk == 0) | (blk == cur) | (blk == cur - 1)
    score = jnp.where(causal, imp + jnp.where(forced, SEL_FORCE, 0.0), -jnp.inf)
    rank = _block_rank(score, nb)
    sel = (rank < min(SEL_COUNT, nb)) & causal
    mask_t = jnp.where(sel, 0.0, NEG).astype(BF16)
    for h in range(HEADS_PER_GROUP):
        rhs_ref[HEAD_DIM:HEAD_DIM + nb, h * TQ:(h + 1) * TQ] = mask_t

    acc_ref[...] = jnp.zeros_like(acc_ref)
    n_tiles = lax.div(q0 + TQ - 1, TK) + 1

    def slc_step(kt, carry):
        m_run, l_run = carry
        off_s = pl.multiple_of((seq - TQ) - (q0 - kt * TK), SUBLANES)
        s = (jnp.dot(ks_ref[0, 0, kt], rhs_ref[...], preferred_element_type=F32)
             + ss_ref[0, pl.ds(off_s, TK), :])
        m_new = jnp.maximum(m_run, jnp.max(s, axis=0, keepdims=True))
        alpha = jnp.exp2(m_run - m_new)
        p = jnp.exp2(s - m_new)
        l_new = alpha * l_run + jnp.sum(p, axis=0, keepdims=True)
        acc_ref[...] = acc_ref[...] * alpha + jnp.dot(vsT_ref[0, 0, kt], p.astype(BF16),
                                                     preferred_element_type=F32)
        return m_new, l_new

    _, l_s = lax.fori_loop(0, n_tiles, slc_step,
                           (jnp.full((1, hq), -jnp.inf, F32), jnp.zeros((1, hq), F32)))
    o_slc = acc_ref[...] * (1.0 / l_s)

    wk = NSA_WINDOW + TQ
    start = pl.multiple_of(jnp.maximum(q0 - NSA_WINDOW, 0), LANES)
    off_w = pl.multiple_of(NSA_WINDOW - (q0 - start), SUBLANES)
    s = (jnp.dot(kw_ref[0, 0, pl.ds(start, wk), :], rhs_q, preferred_element_type=F32)
         + sw_ref[0, pl.ds(off_w, wk), :])
    m, p, l = _softmax_cols(s)
    o_win = jnp.dot(vwT_ref[0, 0, :, pl.ds(start, wk)], p.astype(BF16),
                    preferred_element_type=F32) * (1.0 / l)

    for h in range(HEADS_PER_GROUP):
        sl = slice(h * TQ, (h + 1) * TQ)
        o = (gT_ref[0, 0, 3 * h:3 * h + 1, :] * o_cmp[:, sl]
             + gT_ref[0, 0, 3 * h + 1:3 * h + 2, :] * o_slc[:, sl]
             + gT_ref[0, 0, 3 * h + 2:3 * h + 3, :] * o_win[:, sl])
        o_ref[0, h * HEAD_DIM:(h + 1) * HEAD_DIM, :] = o.astype(o_ref.dtype)


def _nsa_attention(qT, gT, kc, vcT, ks, vsT, kw, vwT, strip_c, strip_s, strip_w, ovT):
    b, _, seq = qT.shape
    nb, ncp = ovT.shape
    hq = HEADS_PER_GROUP * TQ
    gd = HEADS_PER_GROUP * HEAD_DIM
    nkt = seq // TK
    return pl.pallas_call(
        functools.partial(_nsa_kernel, seq=seq, nb=nb, ncp=ncp),
        grid=(KV_GROUPS, b, seq // TQ),
        in_specs=[
            pl.BlockSpec((1, gd, TQ), lambda g, i, q: (i, g, q)),
            pl.BlockSpec((1, 1, 3 * HEADS_PER_GROUP, TQ), lambda g, i, q: (i, g, 0, q)),
            pl.BlockSpec((1, 1, ncp, HEAD_DIM), lambda g, i, q: (i, g, 0, 0)),
            pl.BlockSpec((1, 1, HEAD_DIM, ncp), lambda g, i, q: (i, g, 0, 0)),
            pl.BlockSpec((1, 1, nkt, TK, MXU_DEPTH), lambda g, i, q: (i, g, 0, 0, 0)),
            pl.BlockSpec((1, 1, nkt, HEAD_DIM, TK), lambda g, i, q: (i, g, 0, 0, 0)),
            pl.BlockSpec((1, 1, seq, HEAD_DIM), lambda g, i, q: (i, g, 0, 0)),
            pl.BlockSpec((1, 1, HEAD_DIM, seq), lambda g, i, q: (i, g, 0, 0)),
            pl.BlockSpec((1, strip_c.shape[1], hq), lambda g, i, q: (g, 0, 0)),
            pl.BlockSpec((1, strip_s.shape[1], hq), lambda g, i, q: (g, 0, 0)),
            pl.BlockSpec((1, strip_w.shape[1], hq), lambda g, i, q: (g, 0, 0)),
            pl.BlockSpec((nb, ncp), lambda g, i, q: (0, 0)),
        ],
        out_specs=pl.BlockSpec((1, gd, TQ), lambda g, i, q: (i, g, q)),
        out_shape=jax.ShapeDtypeStruct((b, N_HEADS * HEAD_DIM, seq), BF16),
        scratch_shapes=[pltpu.VMEM((MXU_DEPTH, hq), BF16), pltpu.VMEM((HEAD_DIM, hq), F32)],
        compiler_params=_cparams(("arbitrary", "arbitrary", "arbitrary")),
        name="nsa_attention",
    )(qT, gT, kc, vcT, ks, vsT, kw, vwT, strip_c, strip_s, strip_w, ovT)


def _band_kernel(qT_ref, k_ref, vT_ref, st_ref, o_ref, lse_ref, *, n_back):
    q0 = pl.program_id(2) * TQ
    rhs_q = jnp.concatenate(
        [qT_ref[0, h * HEAD_DIM:(h + 1) * HEAD_DIM, :] for h in range(HEADS_PER_GROUP)], axis=1)
    wk = n_back + TQ
    start = pl.multiple_of(jnp.maximum(q0 - n_back, 0), LANES)
    off = pl.multiple_of(n_back - (q0 - start), SUBLANES)
    s = (jnp.dot(k_ref[0, 0, pl.ds(start, wk), :], rhs_q, preferred_element_type=F32)
         + st_ref[0, pl.ds(off, wk), :])
    m, p, l = _softmax_cols(s)
    o = jnp.dot(vT_ref[0, 0, :, pl.ds(start, wk)], p.astype(BF16), preferred_element_type=F32) * (1.0 / l)
    lse = m + jnp.log2(l)
    for h in range(HEADS_PER_GROUP):
        sl = slice(h * TQ, (h + 1) * TQ)
        o_ref[0, h * HEAD_DIM:(h + 1) * HEAD_DIM, :] = o[:, sl].astype(o_ref.dtype)
        lse_ref[0, 0, h:h + 1, :] = lse[:, sl]


def _band_attention(qT, k, vT, strip, *, n_back):
    n, _, length = qT.shape
    hq = HEADS_PER_GROUP * TQ
    gd = HEADS_PER_GROUP * HEAD_DIM
    return pl.pallas_call(
        functools.partial(_band_kernel, n_back=n_back),
        grid=(KV_GROUPS, n, length // TQ),
        in_specs=[
            pl.BlockSpec((1, gd, TQ), lambda g, i, q: (i, g, q)),
            pl.BlockSpec((1, 1, length, HEAD_DIM), lambda g, i, q: (i, g, 0, 0)),
            pl.BlockSpec((1, 1, HEAD_DIM, length), lambda g, i, q: (i, g, 0, 0)),
            pl.BlockSpec((1, strip.shape[1], hq), lambda g, i, q: (g, 0, 0)),
        ],
        out_specs=[
            pl.BlockSpec((1, gd, TQ), lambda g, i, q: (i, g, q)),
            pl.BlockSpec((1, 1, HEADS_PER_GROUP, TQ), lambda g, i, q: (i, g, 0, q)),
        ],
        out_shape=[
            jax.ShapeDtypeStruct((n, N_HEADS * HEAD_DIM, length), BF16),
            jax.ShapeDtypeStruct((n, KV_GROUPS, HEADS_PER_GROUP, length), F32),
        ],
        compiler_params=_cparams(("arbitrary", "arbitrary", "arbitrary")),
        name="band_attention",
    )(qT, k, vT, strip)


def _combine_kernel(o1_ref, o2_ref, o3_ref, l1_ref, l2_ref, l3_ref, o_ref):
    la, lb, lc = l1_ref[0], l2_ref[0], l3_ref[0]
    mx = jnp.maximum(jnp.maximum(la, lb), lc)
    ea, eb, ec = jnp.exp2(la - mx), jnp.exp2(lb - mx), jnp.exp2(lc - mx)
    inv = 1.0 / (ea + eb + ec)
    wa, wb, wc = ea * inv, eb * inv, ec * inv
    for h in range(N_HEADS):
        sl = slice(h * HEAD_DIM, (h + 1) * HEAD_DIM)
        o = (wa[h:h + 1, :] * o1_ref[0, sl, :].astype(F32)
             + wb[h:h + 1, :] * o2_ref[0, sl, :].astype(F32)
             + wc[h:h + 1, :] * o3_ref[0, sl, :].astype(F32))
        o_ref[0, sl, :] = o.astype(o_ref.dtype)


def _combine(outs, lses, *, ts):
    b, hd, seq = outs[0].shape
    o_spec = pl.BlockSpec((1, hd, ts), lambda i, t: (i, 0, t))
    l_spec = pl.BlockSpec((1, N_HEADS, ts), lambda i, t: (i, 0, t))
    return pl.pallas_call(
        _combine_kernel,
        grid=(b, seq // ts),
        in_specs=[o_spec, o_spec, o_spec, l_spec, l_spec, l_spec],
        out_specs=o_spec,
        out_shape=jax.ShapeDtypeStruct((b, hd, seq), BF16),
        compiler_params=_cparams(("arbitrary", "arbitrary")),
        name="dilated_combine",
    )(*outs, *lses)


def _block_overlap_t(nb, ncp):
    start = np.arange(ncp)[:, None] * CMP_STRIDE
    blk_start = np.arange(nb)[None, :] * SEL_BLOCK
    ov = np.minimum(start + CMP_BLOCK, blk_start + SEL_BLOCK) - np.maximum(start, blk_start)
    ov = np.maximum(ov, 0).astype(np.float32) / CMP_BLOCK
    ov[ncp - 1] = 0.0
    return jnp.asarray(ov.T, BF16)


def _nsa_layer(xf, b, seq, gamma, rel_t, w_in, q_gain, k_gains, cmp_pos, cmp_w1, cmp_w2, tm_proj):
    dh, g = HEAD_DIM, KV_GROUPS
    q_dim, kv_dim = N_HEADS * dh, g * dh
    n_main = q_dim + 6 * kv_dim
    ones, zeros = jnp.ones((kv_dim,), F32), jnp.zeros((kv_dim,), F32)
    col_gain = jnp.concatenate([jnp.tile(q_gain, N_HEADS) * (SCALE * LOG2E), ones, ones,
                                jnp.tile(k_gains[1], g), ones, jnp.tile(k_gains[2], g), ones])
    col_flag = jnp.concatenate([jnp.ones((q_dim,), F32), zeros, zeros, ones, zeros, ones, zeros])
    proj = _norm_proj(xf, gamma, w_in[:, :n_main].astype(BF16), col_gain, col_flag,
                      epilogue="headnorm", out_dtype=BF16, tm=tm_proj, tn=1024, name="nsa_in_proj")
    w_gate = jnp.pad(w_in[:, n_main:], ((0, 0), (0, LANES - 3 * N_HEADS))).astype(BF16)
    gates = _norm_proj(xf, gamma, w_gate, jnp.ones((LANES,), F32), jnp.zeros((LANES,), F32),
                       epilogue="sigmoid", out_dtype=F32, tm=tm_proj, tn=LANES, name="nsa_gate_proj")

    proj = proj.reshape(b, seq, n_main)
    qT = proj[..., :q_dim].transpose(0, 2, 1)
    kvs = proj[..., q_dim:].reshape(b, seq, 6, g, dh)
    gT = gates[:, :3 * N_HEADS].reshape(b, seq, g, 3 * HEADS_PER_GROUP).transpose(0, 2, 3, 1)

    ncp = seq // CMP_STRIDE
    xr = kvs[:, :, 0:2].reshape(b, ncp, CMP_STRIDE, 2, g, dh).transpose(3, 0, 4, 1, 2, 5)
    xr = xr.reshape(2, b, g, ncp, CMP_STRIDE * dh)
    half = CMP_STRIDE * dh
    w1cat = jnp.concatenate([cmp_w1[:, :half], cmp_w1[:, half:]], axis=2).astype(BF16)
    pos2 = jnp.pad(cmp_pos.reshape(2, 2, half), ((0, 0), (0, SUBLANES - 2), (0, 0))).astype(BF16)
    c_gain = jnp.stack([k_gains[0], jnp.ones((dh,), F32)]).reshape(2, 1, dh)
    c_flag = jnp.stack([jnp.ones((dh,), F32), jnp.zeros((dh,), F32)]).reshape(2, 1, dh)
    cmp = _compress(xr, pos2, w1cat, cmp_w2.astype(BF16), c_gain, c_flag)
    kc, vcT = cmp[0], cmp[1].transpose(0, 1, 3, 2)

    nb = seq // SEL_BLOCK
    onehot = np.zeros((seq, MXU_DEPTH - dh), np.float32)
    onehot[np.arange(seq), np.arange(seq) // SEL_BLOCK] = 1.0
    k_slc = kvs[:, :, 2].transpose(0, 2, 1, 3)
    ks = jnp.concatenate([k_slc, jnp.broadcast_to(jnp.asarray(onehot, BF16), (b, g, seq, MXU_DEPTH - dh))],
                         axis=-1).reshape(b, g, seq // TK, TK, MXU_DEPTH)
    vsT = kvs[:, :, 3].reshape(b, seq // TK, TK, g, dh).transpose(0, 3, 1, 4, 2)
    kw = kvs[:, :, 4].transpose(0, 2, 1, 3)
    vwT = kvs[:, :, 5].transpose(0, 2, 3, 1)

    d_s = _toeplitz_dist(seq - TQ + TK, TQ, seq - TQ)
    strip_s = _bias_strip(rel_t, d_s, d_s >= 0, TQ)
    d_w = _toeplitz_dist(2 * NSA_WINDOW + TQ, TQ, NSA_WINDOW)
    strip_w = _bias_strip(rel_t, d_w, (d_w >= 0) & (d_w < NSA_WINDOW), TQ)
    c_c = (seq - TQ) // CMP_STRIDE
    r = np.arange(ncp + c_c, dtype=np.int32)[:, None]
    d_c = np.arange(TQ, dtype=np.int32)[None, :] - CMP_STRIDE * (r - c_c) - (CMP_BLOCK - 1)
    strip_c = _bias_strip(rel_t, d_c, d_c >= 0, TQ)

    return _nsa_attention(qT, gT, kc, vcT, ks, vsT, kw, vwT, strip_c, strip_s, strip_w,
                          _block_overlap_t(nb, ncp))


def _dilated_layer(xf, b, seq, gamma, rel_t, k_sh, v_sh, w_q, q_gains, tm_proj):
    dh = HEAD_DIM
    q_dim = N_HEADS * dh
    n_grp = len(DIL_PAIRS)
    col_gain = jnp.concatenate([jnp.tile(q_gains[i], N_HEADS) for i in range(n_grp)]) * (SCALE * LOG2E)
    q = _norm_proj(xf, gamma, w_q.astype(BF16), col_gain, jnp.ones((n_grp * q_dim,), F32),
                   epilogue="headnorm", out_dtype=BF16, tm=tm_proj, tn=1024, name="dil_q_proj")
    q = q.reshape(b, seq, n_grp, q_dim)
    outs, lses = [], []
    for gi, (window, dil) in enumerate(DIL_PAIRS):
        n_back = window // dil
        length = seq // dil
        qT = q[:, :, gi].reshape(b, length, dil, q_dim).transpose(0, 2, 3, 1).reshape(b * dil, q_dim, length)
        k = k_sh.reshape(b, length, dil, KV_GROUPS, dh).transpose(0, 2, 3, 1, 4).reshape(b * dil, KV_GROUPS, length, dh)
        vT = v_sh.reshape(b, length, dil, KV_GROUPS, dh).transpose(0, 2, 3, 4, 1).reshape(b * dil, KV_GROUPS, dh, length)
        steps = _toeplitz_dist(2 * n_back + TQ, TQ, n_back)
        strip = _bias_strip(rel_t, steps * dil, (steps >= 0) & (steps <= n_back), TQ)
        o, lse = _band_attention(qT, k, vT, strip, n_back=n_back)
        outs.append(o.reshape(b, dil, q_dim, length).transpose(0, 2, 3, 1).reshape(b, q_dim, seq))
        lses.append(lse.reshape(b, dil, N_HEADS, length).transpose(0, 2, 3, 1).reshape(b, N_HEADS, seq))
    return _combine(outs, lses, ts=512)


def _conv_ffn(xf, seq, gamma, w_up, conv_w, w_down, tm):
    act = _ffn_up(xf, gamma, w_up.astype(BF16), conv_w, seq=seq, tm=tm, tn=512, name="ffn_up")
    return _mm_res(act, w_down.astype(BF16), xf, trans_a=False, tm=tm, tn=1024, name="ffn_down")


def kernel(x, rel_bias, attn_norm, ffn_norm, a_w_in, a_q_norm, a_k_norm, a_cmp_pos, a_cmp_w1, a_cmp_w2, a_w_out, kv_norm, kv_w, kv_k_norm, b_w_q, b_q_norm, b_w_out, ffn_w_up, ffn_conv, ffn_w_down):
    b, seq, d = x.shape
    tm = 512
    xf = x.reshape(b * seq, d)
    rel_t = rel_bias.T

    oT = _nsa_layer(xf, b, seq, attn_norm[0], rel_t, a_w_in[0], a_q_norm[0], a_k_norm[0],
                    a_cmp_pos[0], a_cmp_w1[0], a_cmp_w2[0], tm)
    xf = _mm_res(oT, a_w_out[0].astype(BF16), xf, trans_a=True, tm=tm, tn=1024, name="nsa_out_proj")
    xf = _conv_ffn(xf, seq, ffn_norm[0], ffn_w_up[0], ffn_conv[0], ffn_w_down[0], tm)

    kv_dim = KV_GROUPS * HEAD_DIM
    kv_gain = jnp.concatenate([jnp.tile(kv_k_norm, KV_GROUPS), jnp.ones((kv_dim,), F32)])
    kv_flag = jnp.concatenate([jnp.ones((kv_dim,), F32), jnp.zeros((kv_dim,), F32)])
    kv = _norm_proj(xf, kv_norm, kv_w.astype(BF16), kv_gain, kv_flag, epilogue="headnorm",
                    out_dtype=BF16, tm=tm, tn=1024, name="shared_kv_proj")
    kv = kv.reshape(b, seq, 2, KV_GROUPS, HEAD_DIM)

    oT = _dilated_layer(xf, b, seq, attn_norm[1], rel_t, kv[:, :, 0], kv[:, :, 1], b_w_q[0], b_q_norm[0], tm)
    xf = _mm_res(oT, b_w_out[0].astype(BF16), xf, trans_a=True, tm=tm, tn=1024, name="dil_out_proj")
    xf = _conv_ffn(xf, seq, ffn_norm[1], ffn_w_up[1], ffn_conv[1], ffn_w_down[1], tm)
    return xf.reshape(b, seq, d)
```
